```python
import math
import jax, jax.numpy as jnp
from jax import lax
import numpy as np

D_MODEL = 1024
BATCH = 16
SEQ = 2048
DEPTH = 2

CHUNK = 64
A_HEADS = 8
A_HEAD_DIM = 64
A_WIDTH = A_HEADS * A_HEAD_DIM
A_LEFT_CHUNKS = 8
A_BAND = (A_LEFT_CHUNKS + 1) * CHUNK
REL_CLIP = 128
B_WIDTH = D_MODEL - A_WIDTH
CONV_WIDTH = 31
C_HEADS = 4
C_KEY_DIM = D_MODEL // 2
C_VAL_DIM = D_MODEL
C_DK = C_KEY_DIM // C_HEADS
C_DV = C_VAL_DIM // C_HEADS
GATE_RANK = 16
GATE_TEMP = 16.0
D_FF = -(-8 * D_MODEL // (3 * 256)) * 256
ALPHA = (2 * DEPTH) ** 0.25
BETA = (8 * DEPTH) ** -0.25
LN_EPS = 1e-5
RMS_EPS = 1e-6
N_EVEN = (DEPTH + 1) // 2
N_ODD = DEPTH // 2
NEG_INF = -1e30

kernel_name = "hybrid_chunk_attn_conv_gla_deepnorm"


def layer_norm(x, g, b):
    xf = x.astype(jnp.float32)
    mu = jnp.mean(xf, axis=-1, keepdims=True)
    var = jnp.mean(jnp.square(xf - mu), axis=-1, keepdims=True)
    y = (xf - mu) * lax.rsqrt(var + LN_EPS)
    return (y * g.astype(jnp.float32) + b.astype(jnp.float32)).astype(x.dtype)


def chunked_relpos_attention(q, k, v, rel_bias):
    bsz, seq, h, dh = q.shape
    nc = seq // CHUNK
    qc = q.reshape(bsz, nc, CHUNK, h, dh)
    pad = ((0, 0), (A_LEFT_CHUNKS, 0), (0, 0), (0, 0), (0, 0))
    kp = jnp.pad(k.reshape(bsz, nc, CHUNK, h, dh), pad)
    vp = jnp.pad(v.reshape(bsz, nc, CHUNK, h, dh), pad)
    kb = jnp.concatenate([kp[:, o:o + nc] for o in range(A_LEFT_CHUNKS + 1)], axis=2)
    vb = jnp.concatenate([vp[:, o:o + nc] for o in range(A_LEFT_CHUNKS + 1)], axis=2)
    scores = jnp.einsum('bnqhd,bnkhd->bnhqk', qc, kb).astype(jnp.float32) * (dh ** -0.5)
    qi = jnp.arange(CHUNK)[:, None] + A_LEFT_CHUNKS * CHUNK
    kj = jnp.arange(A_BAND)[None, :]
    rel = jnp.clip(qi - kj, -REL_CLIP, REL_CLIP) + REL_CLIP
    bias = rel_bias.astype(jnp.float32)[:, rel]
    key_chunk = jnp.arange(nc)[:, None] - A_LEFT_CHUNKS + (jnp.arange(A_BAND) // CHUNK)[None, :]
    valid = (key_chunk >= 0)[None, :, None, None, :]
    scores = jnp.where(valid, scores + bias[None, None], NEG_INF)
    probs = jax.nn.softmax(scores, axis=-1).astype(v.dtype)
    out = jnp.einsum('bnhqk,bnkhd->bnqhd', probs, vb)
    return out.reshape(bsz, seq, h * dh)


def conformer_conv(u, w_dw, b_dw, g_n, b_n):
    a, gate = jnp.split(u, 2, axis=-1)
    h = a * jax.nn.sigmoid(gate)
    hp = jnp.pad(h, ((0, 0), (CONV_WIDTH - 1, 0), (0, 0)))
    y = lax.conv_general_dilated(
        hp, w_dw[:, None, :].astype(hp.dtype), window_strides=(1,), padding='VALID',
        dimension_numbers=('NWC', 'WIO', 'NWC'), feature_group_count=B_WIDTH)
    y = y + b_dw
    y = layer_norm(y, g_n, b_n)
    return jax.nn.silu(y)


def gla_chunked(q, k, v, log_a):
    bsz, seq, h, dk = q.shape
    dv = v.shape[-1]
    nc = seq // CHUNK

    def to_chunks(t):
        return t.reshape(bsz, nc, CHUNK, h, t.shape[-1]).transpose(1, 0, 3, 2, 4).astype(jnp.float32)

    qc = to_chunks(q) * (dk ** -0.5)
    kc, vc, ac = to_chunks(k), to_chunks(v), to_chunks(log_a)
    causal = jnp.tril(jnp.ones((CHUNK, CHUNK), dtype=bool))[:, :, None]

    def step(state, inp):
        qi, ki, vi, ai = inp
        b = jnp.cumsum(ai, axis=-2)
        diff = b[:, :, :, None, :] - b[:, :, None, :, :]
        decay = jnp.where(causal, jnp.exp(jnp.minimum(diff, 0.0)), 0.0)
        attn = jnp.einsum('bhid,bhjd,bhijd->bhij', qi, ki, decay)
        intra = jnp.einsum('bhij,bhje->bhie', attn, vi)
        inter = jnp.einsum('bhid,bhde->bhie', qi * jnp.exp(b), state)
        b_last = b[:, :, -1:, :]
        k_dec = ki * jnp.exp(b_last - b)
        new_state = state * jnp.exp(b_last[:, :, 0, :])[..., None] + jnp.einsum('bhjd,bhje->bhde', k_dec, vi)
        return new_state, intra + inter

    state0 = jnp.zeros((bsz, h, dk, dv), jnp.float32)
    _, out = lax.scan(step, state0, (qc, kc, vc, ac))
    return out.transpose(1, 0, 3, 2, 4).reshape(bsz, seq, h, dv)


def even_mixer(x, w_in, rel_bias, conv_w, conv_b, conv_g, conv_nb, w_out):
    bsz, seq, _ = x.shape
    u = x @ w_in
    q, k, v, cv = jnp.split(u, [A_WIDTH, 2 * A_WIDTH, 3 * A_WIDTH], axis=-1)
    shp = (bsz, seq, A_HEADS, A_HEAD_DIM)
    a_out = chunked_relpos_attention(q.reshape(shp), k.reshape(shp), v.reshape(shp), rel_bias)
    b_out = conformer_conv(cv, conv_w, conv_b, conv_g, conv_nb)
    return jnp.concatenate([a_out, b_out], axis=-1) @ w_out


def odd_mixer(x, w_in, gate_w, gate_b, head_g, w_out):
    bsz, seq, _ = x.shape
    u = x @ w_in
    q, k, v, g, z = jnp.split(
        u, [C_KEY_DIM, 2 * C_KEY_DIM, 2 * C_KEY_DIM + C_VAL_DIM, 2 * C_KEY_DIM + 2 * C_VAL_DIM], axis=-1)
    log_a = jax.nn.log_sigmoid((z @ gate_w + gate_b).astype(jnp.float32)) / GATE_TEMP
    kshp = (bsz, seq, C_HEADS, C_DK)
    o = gla_chunked(q.reshape(kshp), k.reshape(kshp), v.reshape(bsz, seq, C_HEADS, C_DV), log_a.reshape(kshp))
    o = o * lax.rsqrt(jnp.mean(jnp.square(o), axis=-1, keepdims=True) + RMS_EPS)
    o = o.reshape(bsz, seq, C_VAL_DIM) * head_g.astype(jnp.float32)
    o = (o * jax.nn.silu(g.astype(jnp.float32))).astype(x.dtype)
    return o @ w_out


def swiglu(x, w_gate, w_up, w_down):
    return (jax.nn.silu(x @ w_gate) * (x @ w_up)) @ w_down


def setup_inputs(seed: int = 0) -> dict:
    key = jax.random.key(seed)
    ks = jax.random.split(key, 24)
    f32 = jnp.float32
    nrm = lambda k, shp, s: jax.random.normal(k, shp, f32) * s
    in_even = 3 * A_WIDTH + 2 * B_WIDTH
    in_odd = 2 * C_KEY_DIM + 2 * C_VAL_DIM + GATE_RANK
    return {
        "x": nrm(ks[0], (BATCH, SEQ, D_MODEL), 1.0),
        "even_w_in": nrm(ks[1], (N_EVEN, D_MODEL, in_even), D_MODEL ** -0.5),
        "even_rel_bias": nrm(ks[2], (N_EVEN, A_HEADS, 2 * REL_CLIP + 1), 0.3),
        "even_conv_w": nrm(ks[3], (N_EVEN, CONV_WIDTH, B_WIDTH), CONV_WIDTH ** -0.5),
        "even_conv_b": nrm(ks[4], (N_EVEN, B_WIDTH), 0.01),
        "even_conv_norm_g": 1.0 + nrm(ks[5], (N_EVEN, B_WIDTH), 0.01),
        "even_conv_norm_b": nrm(ks[6], (N_EVEN, B_WIDTH), 0.01),
        "even_w_out": nrm(ks[7], (N_EVEN, A_WIDTH + B_WIDTH, D_MODEL), BETA * (A_WIDTH + B_WIDTH) ** -0.5),
        "odd_w_in": nrm(ks[8], (N_ODD, D_MODEL, in_odd), D_MODEL ** -0.5),
        "odd_gate_w": nrm(ks[9], (N_ODD, GATE_RANK, C_KEY_DIM), GATE_RANK ** -0.5),
        "odd_gate_b": nrm(ks[10], (N_ODD, C_KEY_DIM), 0.01),
        "odd_head_norm_g": 1.0 + nrm(ks[11], (N_ODD, C_VAL_DIM), 0.01),
        "odd_w_out": nrm(ks[12], (N_ODD, C_VAL_DIM, D_MODEL), BETA * C_VAL_DIM ** -0.5),
        "mix_norm_g": 1.0 + nrm(ks[13], (DEPTH, D_MODEL), 0.01),
        "mix_norm_b": nrm(ks[14], (DEPTH, D_MODEL), 0.01),
        "ffn_w_gate": nrm(ks[15], (DEPTH, D_MODEL, D_FF), D_MODEL ** -0.5),
        "ffn_w_up": nrm(ks[16], (DEPTH, D_MODEL, D_FF), D_MODEL ** -0.5),
        "ffn_w_down": nrm(ks[17], (DEPTH, D_FF, D_MODEL), BETA * D_FF ** -0.5),
        "ffn_norm_g": 1.0 + nrm(ks[18], (DEPTH, D_MODEL), 0.01),
        "ffn_norm_b": nrm(ks[19], (DEPTH, D_MODEL), 0.01),
    }


def reference(x, even_w_in, even_rel_bias, even_conv_w, even_conv_b, even_conv_norm_g,
              even_conv_norm_b, even_w_out, odd_w_in, odd_gate_w, odd_gate_b,
              odd_head_norm_g, odd_w_out, mix_norm_g, mix_norm_b, ffn_w_gate, ffn_w_up,
              ffn_w_down, ffn_norm_g, ffn_norm_b):
    for l in range(DEPTH):
        i = l // 2
        if l % 2 == 0:
            m = even_mixer(x, even_w_in[i], even_rel_bias[i], even_conv_w[i], even_conv_b[i],
                           even_conv_norm_g[i], even_conv_norm_b[i], even_w_out[i])
        else:
            m = odd_mixer(x, odd_w_in[i], odd_gate_w[i], odd_gate_b[i], odd_head_norm_g[i], odd_w_out[i])
        x = layer_norm(ALPHA * x + m, mix_norm_g[l], mix_norm_b[l])
        f = swiglu(x, ffn_w_gate[l], ffn_w_up[l], ffn_w_down[l])
        x = layer_norm(ALPHA * x + f, ffn_norm_g[l], ffn_norm_b[l])
    return x
```

```python
import functools

import numpy as np
import jax
import jax.numpy as jnp
from jax import lax
from jax.experimental import pallas as pl
from jax.experimental.pallas import tpu as pltpu

D_MODEL = 1024
CHUNK = 64
A_HEADS = 8
A_HEAD_DIM = 64
A_WIDTH = A_HEADS * A_HEAD_DIM
A_LEFT_CHUNKS = 8
A_BAND = (A_LEFT_CHUNKS + 1) * CHUNK
A_LEFT = A_LEFT_CHUNKS * CHUNK
REL_CLIP = 128
B_WIDTH = D_MODEL - A_WIDTH
CONV_WIDTH = 31
C_HEADS = 4
C_KEY_DIM = D_MODEL // 2
C_VAL_DIM = D_MODEL
C_DK = C_KEY_DIM // C_HEADS
C_DV = C_VAL_DIM // C_HEADS
GATE_RANK = 16
GATE_TEMP = 16.0
D_FF = 2816
DEPTH = 2
ALPHA = (2 * DEPTH) ** 0.25
LN_EPS = 1e-5
RMS_EPS = 1e-6
NEG_INF = -1e30

LANES = 128
TM = 512
CHUNKS_PER_TILE = TM // CHUNK
CONV_PREV = 32
FF_CHUNK = 256
GLA_LEVELS = 6
VMEM_LIMIT = 56 * 1024 * 1024

F32 = jnp.float32
BF16 = jnp.bfloat16


def _layer_norm(y, g, b):
    mu = jnp.mean(y, axis=-1, keepdims=True)
    d = y - mu
    var = jnp.mean(d * d, axis=-1, keepdims=True)
    return d * lax.rsqrt(var + LN_EPS) * g + b


def _dot(a, b):
    return jnp.dot(a, b, preferred_element_type=F32)


def _dot_nt(a, b):
    return lax.dot_general(a, b, (((1,), (1,)), ((), ())), preferred_element_type=F32)


def _dot_tn(a, b):
    return lax.dot_general(a, b, (((0,), (0,)), ((), ())), preferred_element_type=F32)


def _full(shape):
    return pl.BlockSpec(shape, lambda i: (0,) * len(shape))


def _rows(width, rows=TM):
    return pl.BlockSpec((rows, width), lambda i: (i, 0))


def _params(semantics="parallel"):
    return pltpu.CompilerParams(dimension_semantics=(semantics,), vmem_limit_bytes=VMEM_LIMIT)


def _in0_kernel(x_ref, w_ref, q_ref, k_ref, v_ref, h_ref):
    xb = x_ref[...].astype(BF16)

    def proj(c):
        return _dot(xb, w_ref[:, c * A_WIDTH:(c + 1) * A_WIDTH])

    q_ref[...] = proj(0).astype(BF16)
    k_ref[...] = proj(1).astype(BF16)
    v_ref[...] = proj(2).astype(BF16)
    h_ref[...] = proj(3) * jax.nn.sigmoid(proj(4))


def _in0(x, w):
    t = x.shape[0]
    n_in = w.shape[1]
    return pl.pallas_call(
        _in0_kernel,
        grid=(t // TM,),
        in_specs=[_rows(D_MODEL), _full((D_MODEL, n_in))],
        out_specs=[_rows(A_WIDTH), _rows(A_WIDTH), _rows(A_WIDTH), _rows(B_WIDTH)],
        out_shape=[jax.ShapeDtypeStruct((t, A_WIDTH), BF16)] * 3 + [jax.ShapeDtypeStruct((t, B_WIDTH), F32)],
        compiler_params=_params(),
        name="in_proj0",
    )(x, w)


def _mix0_kernel(tiles_per_seq, q_ref, kc_ref, kp_ref, vc_ref, vp_ref, hc_ref, hp_ref, x_ref, bias_ref,
                 cw_ref, cb_ref, cg_ref, cnb_ref, wout_ref, ng_ref, nb_ref, o_ref,
                 kband, vband, hext, cat):
    seq_tile = pl.program_id(0) % tiles_per_seq
    at_start = seq_tile == 0

    kband[0:A_LEFT, :] = kp_ref[...]
    kband[A_LEFT:, :] = kc_ref[...]
    vband[0:A_LEFT, :] = vp_ref[...]
    vband[A_LEFT:, :] = vc_ref[...]
    hext[0:CONV_PREV, :] = jnp.where(at_start, 0.0, hp_ref[...])
    hext[CONV_PREV:, :] = hc_ref[...]

    col = lax.broadcasted_iota(jnp.int32, (2 * CHUNK, A_BAND), 1)
    low_half = lax.broadcasted_iota(jnp.int32, (CHUNK, LANES), 1) < A_HEAD_DIM
    scale = A_HEAD_DIM ** -0.5

    def chunk_body(c, carry):
        r0 = pl.multiple_of(c * CHUNK, CHUNK)
        first_valid = jnp.where(at_start, A_LEFT - c * CHUNK, 0)
        valid = col >= first_valid
        outs = []
        for p in range(A_HEADS // 2):
            ls = slice(p * LANES, (p + 1) * LANES)
            qp = q_ref[pl.ds(r0, CHUNK), ls]
            zero = jnp.zeros_like(qp)
            qs = jnp.concatenate([jnp.where(low_half, qp, zero), jnp.where(low_half, zero, qp)], axis=0)
            kb = kband[pl.ds(r0, A_BAND), ls]
            vb = vband[pl.ds(r0, A_BAND), ls]
            s = _dot_nt(qs, kb) * scale + bias_ref[p]
            s = jnp.where(valid, s, NEG_INF)
            m = jnp.max(s, axis=-1, keepdims=True)
            e = jnp.exp(s - m)
            l = jnp.sum(e, axis=-1, keepdims=True)
            o = _dot(e.astype(BF16), vb) / l
            outs.append(jnp.where(low_half, o[:CHUNK], o[CHUNK:]))
        cat[pl.ds(r0, CHUNK), 0:A_WIDTH] = jnp.concatenate(outs, axis=1).astype(BF16)
        return carry

    lax.fori_loop(0, CHUNKS_PER_TILE, chunk_body, 0)

    off = CONV_PREV - (CONV_WIDTH - 1)
    for rb in range(CHUNKS_PER_TILE):
        base = rb * CHUNK + off
        acc = hext[base:base + CHUNK, :] * cw_ref[0:1, :] + cb_ref[...]
        for w in range(1, CONV_WIDTH):
            acc = acc + hext[base + w:base + w + CHUNK, :] * cw_ref[w:w + 1, :]
        y = _layer_norm(acc, cg_ref[...], cnb_ref[...])
        y = y * jax.nn.sigmoid(y)
        cat[rb * CHUNK:(rb + 1) * CHUNK, A_WIDTH:] = y.astype(BF16)

    m = _dot(cat[...], wout_ref[...])
    o_ref[...] = _layer_norm(ALPHA * x_ref[...] + m, ng_ref[...], nb_ref[...])


def _mix0(q, k, v, h, x, bias, cw, cb, cg, cnb, wout, ng, nb, tiles_per_seq):
    t = x.shape[0]
    prev = pl.BlockSpec((TM, A_WIDTH), lambda i: (jnp.maximum(i - 1, 0), 0))
    hprev = pl.BlockSpec((CONV_PREV, B_WIDTH), lambda i: (jnp.maximum(i * (TM // CONV_PREV) - 1, 0), 0))
    return pl.pallas_call(
        functools.partial(_mix0_kernel, tiles_per_seq),
        grid=(t // TM,),
        in_specs=[_rows(A_WIDTH), _rows(A_WIDTH), prev, _rows(A_WIDTH), prev, _rows(B_WIDTH), hprev,
                  _rows(D_MODEL), _full(bias.shape), _full(cw.shape), _full(cb.shape), _full(cg.shape),
                  _full(cnb.shape), _full(wout.shape), _full(ng.shape), _full(nb.shape)],
        out_specs=_rows(D_MODEL),
        out_shape=jax.ShapeDtypeStruct((t, D_MODEL), F32),
        scratch_shapes=[pltpu.VMEM((A_LEFT + TM, A_WIDTH), BF16), pltpu.VMEM((A_LEFT + TM, A_WIDTH), BF16),
                        pltpu.VMEM((CONV_PREV + TM, B_WIDTH), F32), pltpu.VMEM((TM, D_MODEL), BF16)],
        compiler_params=_params(),
        name="mix0",
    )(q, k, k, v, v, h, h, x, bias, cw, cb, cg, cnb, wout, ng, nb)


def _ffn_kernel(x_ref, wg_ref, wu_ref, wd_ref, g_ref, b_ref, o_ref, hm):
    x = x_ref[...]
    xb = x.astype(BF16)
    for c in range(D_FF // FF_CHUNK):
        sl = slice(c * FF_CHUNK, (c + 1) * FF_CHUNK)
        gate = _dot(xb, wg_ref[:, sl])
        up = _dot(xb, wu_ref[:, sl])
        hm[:, sl] = (gate * jax.nn.sigmoid(gate) * up).astype(BF16)
    f = _dot(hm[...], wd_ref[...])
    o_ref[...] = _layer_norm(ALPHA * x + f, g_ref[...], b_ref[...])


def _ffn(x, wg, wu, wd, g, b):
    t = x.shape[0]
    return pl.pallas_call(
        _ffn_kernel,
        grid=(t // TM,),
        in_specs=[_rows(D_MODEL), _full(wg.shape), _full(wu.shape), _full(wd.shape), _full(g.shape), _full(b.shape)],
        out_specs=_rows(D_MODEL),
        out_shape=jax.ShapeDtypeStruct((t, D_MODEL), F32),
        scratch_shapes=[pltpu.VMEM((TM, D_FF), BF16)],
        compiler_params=_params(),
        name="ffn",
    )(x, wg, wu, wd, g, b)


def _in1_kernel(x_ref, w_ref, wz_ref, gw_ref, gb_ref, q_ref, k_ref, v_ref, g_ref, la_ref):
    xb = x_ref[...].astype(BF16)

    def proj(lo, width):
        return _dot(xb, w_ref[:, lo:lo + width])

    q_ref[...] = proj(0, C_KEY_DIM)
    k_ref[...] = proj(C_KEY_DIM, C_KEY_DIM)
    for c in range(2):
        half = C_VAL_DIM // 2
        v_ref[:, c * half:(c + 1) * half] = proj(2 * C_KEY_DIM + c * half, half)
        g_ref[:, c * half:(c + 1) * half] = proj(2 * C_KEY_DIM + C_VAL_DIM + c * half, half)
    z = _dot(xb, wz_ref[...])
    pre = _dot(z.astype(BF16), gw_ref[...]) + gb_ref[...]
    log_sig = jnp.minimum(pre, 0.0) - jnp.log1p(jnp.exp(-jnp.abs(pre)))
    la_ref[...] = log_sig / GATE_TEMP


def _in1(x, w, wz, gw, gb):
    t = x.shape[0]
    return pl.pallas_call(
        _in1_kernel,
        grid=(t // TM,),
        in_specs=[_rows(D_MODEL), _full(w.shape), _full(wz.shape), _full(gw.shape), _full(gb.shape)],
        out_specs=[_rows(C_KEY_DIM), _rows(C_KEY_DIM), _rows(C_VAL_DIM), _rows(C_VAL_DIM), _rows(C_KEY_DIM)],
        out_shape=[jax.ShapeDtypeStruct((t, C_KEY_DIM), F32), jax.ShapeDtypeStruct((t, C_KEY_DIM), F32),
                   jax.ShapeDtypeStruct((t, C_VAL_DIM), F32), jax.ShapeDtypeStruct((t, C_VAL_DIM), F32),
                   jax.ShapeDtypeStruct((t, C_KEY_DIM), F32)],
        compiler_params=_params(),
        name="in_proj1",
    )(x, w, wz, gw, gb)


def _gla_constants():
    idx = np.arange(CHUNK)
    i, t = idx[:, None], idx[None, :]
    sel = []
    for lvl in range(1, GLA_LEVELS + 1):
        s = 1 << lvl
        sel.append((t >= i - i % s) & (t <= i))
        sel.append((t > i) & (t <= i - i % s + s - 1))
    sel = np.concatenate(sel, axis=0).astype(np.float32)
    masks = []
    for lvl in range(GLA_LEVELS):
        bi, bj = idx[:, None] >> lvl, idx[None, :] >> lvl
        masks.append((bi % 2 == 1) & (bj == bi - 1))
    masks = np.stack(masks).astype(np.float32)
    return jnp.asarray(sel, BF16), jnp.asarray(masks, F32)


def _gla_kernel(tiles_per_seq, q_ref, k_ref, v_ref, g_ref, la_ref, x_ref, sel_ref, mask_ref, hg_ref,
                wout_ref, ng_ref, nb_ref, o_ref, state, obuf):
    @pl.when(pl.program_id(0) % tiles_per_seq == 0)
    def _():
        state[...] = jnp.zeros_like(state)

    q_scale = C_DK ** -0.5

    def chunk_body(c, carry):
        r0 = pl.multiple_of(c * CHUNK, CHUNK)
        rows = pl.ds(r0, CHUNK)
        la = la_ref[rows, :]
        la_hi = la.astype(BF16)
        la_lo = (la - la_hi.astype(F32)).astype(BF16)
        sel = sel_ref[...]
        dsum = _dot(sel, la_hi) + _dot(sel, la_lo)
        for h in range(C_HEADS):
            ks = slice(h * C_DK, (h + 1) * C_DK)
            vs = slice(h * C_DV, (h + 1) * C_DV)
            qh = q_ref[rows, ks] * q_scale
            kh = k_ref[rows, ks]
            vh = v_ref[rows, vs]
            vb = vh.astype(BF16)

            def dq(lvl):
                return dsum[(lvl - 1) * 2 * CHUNK:(lvl - 1) * 2 * CHUNK + CHUNK, ks]

            def dk(lvl):
                return dsum[(lvl - 1) * 2 * CHUNK + CHUNK:lvl * 2 * CHUNK, ks]

            attn = mask_ref[0] * _dot_nt((qh * jnp.exp(la[:, ks])).astype(BF16), kh.astype(BF16))
            for lvl in range(1, GLA_LEVELS):
                qt = (qh * jnp.exp(dq(lvl))).astype(BF16)
                kt = (kh * jnp.exp(dk(lvl))).astype(BF16)
                attn = attn + mask_ref[lvl] * _dot_nt(qt, kt)
            diag = jnp.sum(qh * kh, axis=-1, keepdims=True)
            st = state[h]
            b = dq(GLA_LEVELS)
            o = _dot(attn.astype(BF16), vb) + diag * vh
            o = o + _dot_nt((qh * jnp.exp(b)).astype(BF16), st.astype(BF16))
            k_dec = (kh * jnp.exp(dk(GLA_LEVELS))).astype(BF16)
            state[h] = st * jnp.exp(b[CHUNK - 1:CHUNK, :]) + _dot_tn(vb, k_dec)
            o = o * lax.rsqrt(jnp.mean(o * o, axis=-1, keepdims=True) + RMS_EPS) * hg_ref[:, vs]
            gh = g_ref[rows, vs]
            obuf[rows, vs] = (o * (gh * jax.nn.sigmoid(gh))).astype(BF16)
        return carry

    lax.fori_loop(0, CHUNKS_PER_TILE, chunk_body, 0)

    m = _dot(obuf[...], wout_ref[...])
    o_ref[...] = _layer_norm(ALPHA * x_ref[...] + m, ng_ref[...], nb_ref[...])


def _gla(q, k, v, g, la, x, hg, wout, ng, nb, tiles_per_seq):
    t = x.shape[0]
    sel, masks = _gla_constants()
    return pl.pallas_call(
        functools.partial(_gla_kernel, tiles_per_seq),
        grid=(t // TM,),
        in_specs=[_rows(C_KEY_DIM), _rows(C_KEY_DIM), _rows(C_VAL_DIM), _rows(C_VAL_DIM), _rows(C_KEY_DIM),
                  _rows(D_MODEL), _full(sel.shape), _full(masks.shape), _full(hg.shape), _full(wout.shape),
                  _full(ng.shape), _full(nb.shape)],
        out_specs=_rows(D_MODEL),
        out_shape=jax.ShapeDtypeStruct((t, D_MODEL), F32),
        scratch_shapes=[pltpu.VMEM((C_HEADS, C_DV, C_DK), F32), pltpu.VMEM((TM, C_VAL_DIM), BF16)],
        compiler_params=_params("arbitrary"),
        name="gla",
    )(q, k, v, g, la, x, sel, masks, hg, wout, ng, nb)


def _rel_bias_table(rel_bias):
    qi = jnp.arange(CHUNK)[:, None] + A_LEFT
    kj = jnp.arange(A_BAND)[None, :]
    rel = jnp.clip(qi - kj, -REL_CLIP, REL_CLIP) + REL_CLIP
    return rel_bias[:, rel].reshape(A_HEADS // 2, 2 * CHUNK, A_BAND)


def kernel(x, even_w_in, even_rel_bias, even_conv_w, even_conv_b, even_conv_norm_g, even_conv_norm_b, even_w_out, odd_w_in, odd_gate_w, odd_gate_b, odd_head_norm_g, odd_w_out, mix_norm_g, mix_norm_b, ffn_w_gate, ffn_w_up, ffn_w_down, ffn_norm_g, ffn_norm_b):
    bsz, seq, d = x.shape
    assert d == D_MODEL and seq % TM == 0
    tiles_per_seq = seq // TM
    row = lambda p: p.reshape(1, -1)
    xt = x.reshape(bsz * seq, d)

    q, k, v, h = _in0(xt, even_w_in[0].astype(BF16))
    xt = _mix0(q, k, v, h, xt, _rel_bias_table(even_rel_bias[0]), even_conv_w[0], row(even_conv_b[0]),
               row(even_conv_norm_g[0]), row(even_conv_norm_b[0]), even_w_out[0].astype(BF16),
               row(mix_norm_g[0]), row(mix_norm_b[0]), tiles_per_seq)
    xt = _ffn(xt, ffn_w_gate[0].astype(BF16), ffn_w_up[0].astype(BF16), ffn_w_down[0].astype(BF16),
              row(ffn_norm_g[0]), row(ffn_norm_b[0]))

    n_main = 2 * C_KEY_DIM + 2 * C_VAL_DIM
    w1 = odd_w_in[0]
    wz = jnp.pad(w1[:, n_main:], ((0, 0), (0, LANES - GATE_RANK))).astype(BF16)
    gw = jnp.pad(odd_gate_w[0], ((0, LANES - GATE_RANK), (0, 0))).astype(BF16)
    q, k, v, g, la = _in1(xt, w1[:, :n_main].astype(BF16), wz, gw, row(odd_gate_b[0]))
    xt = _gla(q, k, v, g, la, xt, row(odd_head_norm_g[0]), odd_w_out[0].astype(BF16),
              row(mix_norm_g[1]), row(mix_norm_b[1]), tiles_per_seq)
    xt = _ffn(xt, ffn_w_gate[1].astype(BF16), ffn_w_up[1].astype(BF16), ffn_w_down[1].astype(BF16),
              row(ffn_norm_g[1]), row(ffn_norm_b[1]))
    return xt.reshape(bsz, seq, d)
```

```python
import functools

import numpy as np
import jax
import jax.numpy as jnp
from jax import lax
from jax.experimental import pallas as pl
from jax.experimental.pallas import tpu as pltpu

D_MODEL = 1024
CHUNK = 64
A_HEADS = 8
A_HEAD_DIM = 64
A_WIDTH = A_HEADS * A_HEAD_DIM
A_LEFT_CHUNKS = 8
A_BAND = (A_LEFT_CHUNKS + 1) * CHUNK
A_LEFT = A_LEFT_CHUNKS * CHUNK
REL_CLIP = 128
B_WIDTH = D_MODEL - A_WIDTH
CONV_WIDTH = 31
C_HEADS = 4
C_KEY_DIM = D_MODEL // 2
C_VAL_DIM = D_MODEL
C_DK = C_KEY_DIM // C_HEADS
C_DV = C_VAL_DIM // C_HEADS
GATE_RANK = 16
GATE_TEMP = 16.0
D_FF = 2816
DEPTH = 2
ALPHA = (2 * DEPTH) ** 0.25
LN_EPS = 1e-5
RMS_EPS = 1e-6
NEG_INF = -1e30

LANES = 128
TM = 512
CHUNKS_PER_TILE = TM // CHUNK
CONV_PREV = 32
FF_CHUNK = 256
GLA_LEVELS = 6
GTAB_LEN = 640
HSHIFT_ROWS = TM + CONV_PREV - 8
LOG2E = 1.4426950408889634
VMEM_LIMIT = 56 * 1024 * 1024

F32 = jnp.float32
BF16 = jnp.bfloat16


def _layer_norm(y, g, b):
    mu = jnp.mean(y, axis=-1, keepdims=True)
    d = y - mu
    var = jnp.mean(d * d, axis=-1, keepdims=True)
    return d * lax.rsqrt(var + LN_EPS) * g + b


def _dot(a, b):
    return jnp.dot(a, b, preferred_element_type=F32)


def _dot_nt(a, b):
    return lax.dot_general(a, b, (((1,), (1,)), ((), ())), preferred_element_type=F32)


def _dot_tn(a, b):
    return lax.dot_general(a, b, (((0,), (0,)), ((), ())), preferred_element_type=F32)


def _full(shape):
    return pl.BlockSpec(shape, lambda i: (0,) * len(shape))


def _rows(width, rows=TM):
    return pl.BlockSpec((rows, width), lambda i: (i, 0))


def _params(semantics="parallel"):
    return pltpu.CompilerParams(dimension_semantics=(semantics,), vmem_limit_bytes=VMEM_LIMIT)


def _in0_kernel(x_ref, w_ref, q_ref, k_ref, v_ref, h_ref):
    xb = x_ref[...].astype(BF16)

    def proj(c):
        return _dot(xb, w_ref[:, c * A_WIDTH:(c + 1) * A_WIDTH])

    q_ref[...] = proj(0).astype(BF16)
    k_ref[...] = proj(1).astype(BF16)
    v_ref[...] = proj(2).astype(BF16)
    h_ref[...] = proj(3) * jax.nn.sigmoid(proj(4))


def _in0(x, w):
    t = x.shape[0]
    n_in = w.shape[1]
    return pl.pallas_call(
        _in0_kernel,
        grid=(t // TM,),
        in_specs=[_rows(D_MODEL), _full((D_MODEL, n_in))],
        out_specs=[_rows(A_WIDTH), _rows(A_WIDTH), _rows(A_WIDTH), _rows(B_WIDTH)],
        out_shape=[jax.ShapeDtypeStruct((t, A_WIDTH), BF16)] * 3 + [jax.ShapeDtypeStruct((t, B_WIDTH), F32)],
        compiler_params=_params(),
        name="in_proj0",
    )(x, w)


def _mix0_kernel(tiles_per_seq, q_ref, kc_ref, kp_ref, vc_ref, vp_ref, hc_ref, hp_ref, x_ref, gtab_ref,
                 cw_ref, cb_ref, cg_ref, cnb_ref, wout_ref, ng_ref, nb_ref, o_ref,
                 kband, vband, hext, hshift, cat, bias):
    seq_tile = pl.program_id(0) % tiles_per_seq
    at_start = seq_tile == 0

    @pl.when(pl.program_id(0) == 0)
    def _():
        for h in range(A_HEADS):
            g = jnp.broadcast_to(gtab_ref[h:h + 1, :] * LOG2E, (CHUNK, GTAB_LEN))
            toeplitz = pltpu.roll(g, GTAB_LEN - (CHUNK - 1), 1, stride=1, stride_axis=0)
            bias[h // 2, (h % 2) * CHUNK:(h % 2 + 1) * CHUNK, :] = toeplitz[:, :A_BAND]

    kband[0:A_LEFT, :] = kp_ref[...]
    kband[A_LEFT:, :] = kc_ref[...]
    vband[0:A_LEFT, :] = vp_ref[...]
    vband[A_LEFT:, :] = vc_ref[...]
    hext[0:CONV_PREV, :] = jnp.where(at_start, 0.0, hp_ref[...])
    hext[CONV_PREV:, :] = hc_ref[...]

    col = lax.broadcasted_iota(jnp.int32, (2 * CHUNK, A_BAND), 1)
    low_half = lax.broadcasted_iota(jnp.int32, (CHUNK, LANES), 1) < A_HEAD_DIM
    scale2 = A_HEAD_DIM ** -0.5 * LOG2E
    pairs = [slice(p * LANES, (p + 1) * LANES) for p in range(A_HEADS // 2)]

    def attend(masked):
        def chunk_body(c, carry):
            r0 = pl.multiple_of(c * CHUNK, CHUNK)
            valid = col >= A_LEFT - c * CHUNK
            scores = []
            for ls in pairs:
                qp = q_ref[pl.ds(r0, CHUNK), ls]
                zero = jnp.zeros_like(qp)
                qs = jnp.concatenate([jnp.where(low_half, qp, zero), jnp.where(low_half, zero, qp)], axis=0)
                scores.append(_dot_nt(qs, kband[pl.ds(r0, A_BAND), ls]))
            probs, denoms = [], []
            for p, s in enumerate(scores):
                s = s * scale2 + bias[p]
                if masked:
                    s = jnp.where(valid, s, NEG_INF)
                e = jnp.exp2(s - jnp.max(s, axis=-1, keepdims=True))
                denoms.append(jnp.sum(e, axis=-1, keepdims=True))
                probs.append(e.astype(BF16))
            outs = []
            for ls, e, l in zip(pairs, probs, denoms):
                o = _dot(e, vband[pl.ds(r0, A_BAND), ls]) / l
                outs.append(jnp.where(low_half, o[:CHUNK], o[CHUNK:]))
            cat[pl.ds(r0, CHUNK), 0:A_WIDTH] = jnp.concatenate(outs, axis=1).astype(BF16)
            return carry

        lax.fori_loop(0, CHUNKS_PER_TILE, chunk_body, 0)

    @pl.when(at_start)
    def _():
        attend(True)

    @pl.when(jnp.logical_not(at_start))
    def _():
        attend(False)

    for b in range(1, 8):
        hshift[b - 1] = hext[b:b + HSHIFT_ROWS, :]
    off = CONV_PREV - (CONV_WIDTH - 1)
    for rb in range(CHUNKS_PER_TILE):
        acc = cb_ref[...]
        for w in range(CONV_WIDTH):
            shift, base = (off + w) % 8, rb * CHUNK + (off + w) // 8 * 8
            src = hext if shift == 0 else hshift.at[shift - 1]
            acc = acc + src[base:base + CHUNK, :] * cw_ref[w:w + 1, :]
        y = _layer_norm(acc, cg_ref[...], cnb_ref[...])
        y = y * jax.nn.sigmoid(y)
        cat[rb * CHUNK:(rb + 1) * CHUNK, A_WIDTH:] = y.astype(BF16)

    m = _dot(cat[...], wout_ref[...])
    o_ref[...] = _layer_norm(ALPHA * x_ref[...] + m, ng_ref[...], nb_ref[...])


def _mix0(q, k, v, h, x, gtab, cw, cb, cg, cnb, wout, ng, nb, tiles_per_seq):
    t = x.shape[0]
    prev = pl.BlockSpec((TM, A_WIDTH), lambda i: (jnp.maximum(i - 1, 0), 0))
    hprev = pl.BlockSpec((CONV_PREV, B_WIDTH), lambda i: (jnp.maximum(i * (TM // CONV_PREV) - 1, 0), 0))
    return pl.pallas_call(
        functools.partial(_mix0_kernel, tiles_per_seq),
        grid=(t // TM,),
        in_specs=[_rows(A_WIDTH), _rows(A_WIDTH), prev, _rows(A_WIDTH), prev, _rows(B_WIDTH), hprev,
                  _rows(D_MODEL), _full(gtab.shape), _full(cw.shape), _full(cb.shape), _full(cg.shape),
                  _full(cnb.shape), _full(wout.shape), _full(ng.shape), _full(nb.shape)],
        out_specs=_rows(D_MODEL),
        out_shape=jax.ShapeDtypeStruct((t, D_MODEL), F32),
        scratch_shapes=[pltpu.VMEM((A_LEFT + TM, A_WIDTH), BF16), pltpu.VMEM((A_LEFT + TM, A_WIDTH), BF16),
                        pltpu.VMEM((CONV_PREV + TM, B_WIDTH), F32), pltpu.VMEM((7, HSHIFT_ROWS, B_WIDTH), F32),
                        pltpu.VMEM((TM, D_MODEL), BF16), pltpu.VMEM((A_HEADS // 2, 2 * CHUNK, A_BAND), F32)],
        compiler_params=_params("arbitrary"),
        name="mix0",
    )(q, k, k, v, v, h, h, x, gtab, cw, cb, cg, cnb, wout, ng, nb)


def _ffn_kernel(x_ref, wg_ref, wu_ref, wd_ref, g_ref, b_ref, o_ref, hm):
    x = x_ref[...]
    xb = x.astype(BF16)
    for c in range(D_FF // FF_CHUNK):
        sl = slice(c * FF_CHUNK, (c + 1) * FF_CHUNK)
        gate = _dot(xb, wg_ref[:, sl])
        up = _dot(xb, wu_ref[:, sl])
        hm[:, sl] = (gate * jax.nn.sigmoid(gate) * up).astype(BF16)
    f = _dot(hm[...], wd_ref[...])
    o_ref[...] = _layer_norm(ALPHA * x + f, g_ref[...], b_ref[...])


def _ffn(x, wg, wu, wd, g, b):
    t = x.shape[0]
    return pl.pallas_call(
        _ffn_kernel,
        grid=(t // TM,),
        in_specs=[_rows(D_MODEL), _full(wg.shape), _full(wu.shape), _full(wd.shape), _full(g.shape), _full(b.shape)],
        out_specs=_rows(D_MODEL),
        out_shape=jax.ShapeDtypeStruct((t, D_MODEL), F32),
        scratch_shapes=[pltpu.VMEM((TM, D_FF), BF16)],
        compiler_params=_params(),
        name="ffn",
    )(x, wg, wu, wd, g, b)


def _in1_kernel(x_ref, w_ref, wz_ref, gw_ref, gb_ref, q_ref, k_ref, v_ref, g_ref, la_ref):
    xb = x_ref[...].astype(BF16)

    def proj(lo, width):
        return _dot(xb, w_ref[:, lo:lo + width])

    q_ref[...] = proj(0, C_KEY_DIM)
    k_ref[...] = proj(C_KEY_DIM, C_KEY_DIM)
    for c in range(2):
        half = C_VAL_DIM // 2
        v_ref[:, c * half:(c + 1) * half] = proj(2 * C_KEY_DIM + c * half, half)
        g_ref[:, c * half:(c + 1) * half] = proj(2 * C_KEY_DIM + C_VAL_DIM + c * half, half)
    z = _dot(xb, wz_ref[...])
    pre = _dot(z.astype(BF16), gw_ref[...]) + gb_ref[...]
    log_sig = jnp.minimum(pre, 0.0) - jnp.log1p(jnp.exp(-jnp.abs(pre)))
    la_ref[...] = log_sig / GATE_TEMP


def _in1(x, w, wz, gw, gb):
    t = x.shape[0]
    return pl.pallas_call(
        _in1_kernel,
        grid=(t // TM,),
        in_specs=[_rows(D_MODEL), _full(w.shape), _full(wz.shape), _full(gw.shape), _full(gb.shape)],
        out_specs=[_rows(C_KEY_DIM), _rows(C_KEY_DIM), _rows(C_VAL_DIM), _rows(C_VAL_DIM), _rows(C_KEY_DIM)],
        out_shape=[jax.ShapeDtypeStruct((t, C_KEY_DIM), F32), jax.ShapeDtypeStruct((t, C_KEY_DIM), F32),
                   jax.ShapeDtypeStruct((t, C_VAL_DIM), F32), jax.ShapeDtypeStruct((t, C_VAL_DIM), F32),
                   jax.ShapeDtypeStruct((t, C_KEY_DIM), F32)],
        compiler_params=_params(),
        name="in_proj1",
    )(x, w, wz, gw, gb)


def _gla_constants():
    idx = np.arange(CHUNK)
    i, t = idx[:, None], idx[None, :]
    sel = []
    for lvl in range(1, GLA_LEVELS + 1):
        s = 1 << lvl
        sel.append((t >= i - i % s) & (t <= i))
        sel.append((t > i) & (t <= i - i % s + s - 1))
    sel = np.concatenate(sel, axis=0).astype(np.float32)
    masks = []
    for lvl in range(GLA_LEVELS):
        bi, bj = idx[:, None] >> lvl, idx[None, :] >> lvl
        masks.append((bi % 2 == 1) & (bj == bi - 1))
    masks = np.stack(masks).astype(np.float32)
    return jnp.asarray(sel, BF16), jnp.asarray(masks, F32)


def _gla_kernel(tiles_per_seq, q_ref, k_ref, v_ref, g_ref, la_ref, x_ref, sel_ref, mask_ref, hg_ref,
                wout_ref, ng_ref, nb_ref, o_ref, state, obuf):
    @pl.when(pl.program_id(0) % tiles_per_seq == 0)
    def _():
        state[...] = jnp.zeros_like(state)

    q_scale = C_DK ** -0.5

    def chunk_body(c, carry):
        r0 = pl.multiple_of(c * CHUNK, CHUNK)
        rows = pl.ds(r0, CHUNK)
        la = la_ref[rows, :]
        la_hi = la.astype(BF16)
        la_lo = (la - la_hi.astype(F32)).astype(BF16)
        sel = sel_ref[...]
        dsum = _dot(sel, la_hi) + _dot(sel, la_lo)
        for h in range(C_HEADS):
            ks = slice(h * C_DK, (h + 1) * C_DK)
            vs = slice(h * C_DV, (h + 1) * C_DV)
            qh = q_ref[rows, ks] * q_scale
            kh = k_ref[rows, ks]
            vh = v_ref[rows, vs]
            vb = vh.astype(BF16)

            def dq(lvl):
                return dsum[(lvl - 1) * 2 * CHUNK:(lvl - 1) * 2 * CHUNK + CHUNK, ks]

            def dk(lvl):
                return dsum[(lvl - 1) * 2 * CHUNK + CHUNK:lvl * 2 * CHUNK, ks]

            attn = mask_ref[0] * _dot_nt((qh * jnp.exp(la[:, ks])).astype(BF16), kh.astype(BF16))
            for lvl in range(1, GLA_LEVELS):
                qt = (qh * jnp.exp(dq(lvl))).astype(BF16)
                kt = (kh * jnp.exp(dk(lvl))).astype(BF16)
                attn = attn + mask_ref[lvl] * _dot_nt(qt, kt)
            diag = jnp.sum(qh * kh, axis=-1, keepdims=True)
            st = state[h]
            b = dq(GLA_LEVELS)
            o = _dot(attn.astype(BF16), vb) + diag * vh
            o = o + _dot_nt((qh * jnp.exp(b)).astype(BF16), st.astype(BF16))
            k_dec = (kh * jnp.exp(dk(GLA_LEVELS))).astype(BF16)
            state[h] = st * jnp.exp(b[CHUNK - 1:CHUNK, :]) + _dot_tn(vb, k_dec)
            o = o * lax.rsqrt(jnp.mean(o * o, axis=-1, keepdims=True) + RMS_EPS) * hg_ref[:, vs]
            gh = g_ref[rows, vs]
            obuf[rows, vs] = (o * (gh * jax.nn.sigmoid(gh))).astype(BF16)
        return carry

    lax.fori_loop(0, CHUNKS_PER_TILE, chunk_body, 0)

    m = _dot(obuf[...], wout_ref[...])
    o_ref[...] = _layer_norm(ALPHA * x_ref[...] + m, ng_ref[...], nb_ref[...])


def _gla(q, k, v, g, la, x, hg, wout, ng, nb, tiles_per_seq):
    t = x.shape[0]
    sel, masks = _gla_constants()
    return pl.pallas_call(
        functools.partial(_gla_kernel, tiles_per_seq),
        grid=(t // TM,),
        in_specs=[_rows(C_KEY_DIM), _rows(C_KEY_DIM), _rows(C_VAL_DIM), _rows(C_VAL_DIM), _rows(C_KEY_DIM),
                  _rows(D_MODEL), _full(sel.shape), _full(masks.shape), _full(hg.shape), _full(wout.shape),
                  _full(ng.shape), _full(nb.shape)],
        out_specs=_rows(D_MODEL),
        out_shape=jax.ShapeDtypeStruct((t, D_MODEL), F32),
        scratch_shapes=[pltpu.VMEM((C_HEADS, C_DV, C_DK), F32), pltpu.VMEM((TM, C_VAL_DIM), BF16)],
        compiler_params=_params("arbitrary"),
        name="gla",
    )(q, k, v, g, la, x, sel, masks, hg, wout, ng, nb)


def _rel_bias_diagonals(rel_bias):
    far = jnp.broadcast_to(rel_bias[:, 2 * REL_CLIP:], (rel_bias.shape[0], A_BAND - REL_CLIP))
    near = rel_bias[:, REL_CLIP - (CHUNK - 1):2 * REL_CLIP][:, ::-1]
    return jnp.concatenate([far, near, near[:, -1:]], axis=1)


def kernel(x, even_w_in, even_rel_bias, even_conv_w, even_conv_b, even_conv_norm_g, even_conv_norm_b, even_w_out, odd_w_in, odd_gate_w, odd_gate_b, odd_head_norm_g, odd_w_out, mix_norm_g, mix_norm_b, ffn_w_gate, ffn_w_up, ffn_w_down, ffn_norm_g, ffn_norm_b):
    bsz, seq, d = x.shape
    assert d == D_MODEL and seq % TM == 0
    tiles_per_seq = seq // TM
    row = lambda p: p.reshape(1, -1)
    xt = x.reshape(bsz * seq, d)

    q, k, v, h = _in0(xt, even_w_in[0].astype(BF16))
    xt = _mix0(q, k, v, h, xt, _rel_bias_diagonals(even_rel_bias[0]), even_conv_w[0], row(even_conv_b[0]),
               row(even_conv_norm_g[0]), row(even_conv_norm_b[0]), even_w_out[0].astype(BF16),
               row(mix_norm_g[0]), row(mix_norm_b[0]), tiles_per_seq)
    xt = _ffn(xt, ffn_w_gate[0].astype(BF16), ffn_w_up[0].astype(BF16), ffn_w_down[0].astype(BF16),
              row(ffn_norm_g[0]), row(ffn_norm_b[0]))

    n_main = 2 * C_KEY_DIM + 2 * C_VAL_DIM
    w1 = odd_w_in[0]
    wz = jnp.pad(w1[:, n_main:], ((0, 0), (0, LANES - GATE_RANK))).astype(BF16)
    gw = jnp.pad(odd_gate_w[0], ((0, LANES - GATE_RANK), (0, 0))).astype(BF16)
    q, k, v, g, la = _in1(xt, w1[:, :n_main].astype(BF16), wz, gw, row(odd_gate_b[0]))
    xt = _gla(q, k, v, g, la, xt, row(odd_head_norm_g[0]), odd_w_out[0].astype(BF16),
              row(mix_norm_g[1]), row(mix_norm_b[1]), tiles_per_seq)
    xt = _ffn(xt, ffn_w_gate[1].astype(BF16), ffn_w_up[1].astype(BF16), ffn_w_down[1].astype(BF16),
              row(ffn_norm_g[1]), row(ffn_norm_b[1]))
    return xt.reshape(bsz, seq, d)
```

```python
import functools

import numpy as np
import jax
import jax.numpy as jnp
from jax import lax
from jax.experimental import pallas as pl
from jax.experimental.pallas import tpu as pltpu

D_MODEL = 1024
CHUNK = 64
A_HEADS = 8
A_HEAD_DIM = 64
A_WIDTH = A_HEADS * A_HEAD_DIM
A_LEFT_CHUNKS = 8
A_BAND = (A_LEFT_CHUNKS + 1) * CHUNK
A_LEFT = A_LEFT_CHUNKS * CHUNK
REL_CLIP = 128
B_WIDTH = D_MODEL - A_WIDTH
CONV_WIDTH = 31
C_HEADS = 4
C_KEY_DIM = D_MODEL // 2
C_VAL_DIM = D_MODEL
C_DK = C_KEY_DIM // C_HEADS
C_DV = C_VAL_DIM // C_HEADS
GATE_RANK = 16
GATE_TEMP = 16.0
D_FF = 2816
DEPTH = 2
ALPHA = (2 * DEPTH) ** 0.25
LN_EPS = 1e-5
RMS_EPS = 1e-6
NEG_INF = -1e30

LANES = 128
TM = 512
CHUNKS_PER_TILE = TM // CHUNK
CONV_PREV = 32
FF_CHUNK = 256
GLA_LEVELS = 6
GLA_SEL_LEVELS = (1, 2)
GLA_SEL_ROWS = (1 + 2 * len(GLA_SEL_LEVELS)) * CHUNK
SUBLANES = 8
GTAB_LEN = 640
HSHIFT_ROWS = TM + CONV_PREV - 8
LOG2E = 1.4426950408889634
VMEM_LIMIT = 56 * 1024 * 1024

F32 = jnp.float32
BF16 = jnp.bfloat16


def _layer_norm(y, g, b):
    mu = jnp.mean(y, axis=-1, keepdims=True)
    d = y - mu
    var = jnp.mean(d * d, axis=-1, keepdims=True)
    return d * lax.rsqrt(var + LN_EPS) * g + b


def _dot(a, b):
    return jnp.dot(a, b, preferred_element_type=F32)


def _dot_nt(a, b):
    return lax.dot_general(a, b, (((1,), (1,)), ((), ())), preferred_element_type=F32)


def _dot_tn(a, b):
    return lax.dot_general(a, b, (((0,), (0,)), ((), ())), preferred_element_type=F32)


def _full(shape):
    return pl.BlockSpec(shape, lambda i: (0,) * len(shape))


def _rows(width, rows=TM):
    return pl.BlockSpec((rows, width), lambda i: (i, 0))


def _params(semantics="parallel"):
    return pltpu.CompilerParams(dimension_semantics=(semantics,), vmem_limit_bytes=VMEM_LIMIT)


def _in0_kernel(x_ref, w_ref, q_ref, k_ref, v_ref, h_ref):
    xb = x_ref[...].astype(BF16)

    def proj(c):
        return _dot(xb, w_ref[:, c * A_WIDTH:(c + 1) * A_WIDTH])

    q_ref[...] = proj(0).astype(BF16)
    k_ref[...] = proj(1).astype(BF16)
    v_ref[...] = proj(2).astype(BF16)
    h_ref[...] = proj(3) * jax.nn.sigmoid(proj(4))


def _in0(x, w):
    t = x.shape[0]
    n_in = w.shape[1]
    return pl.pallas_call(
        _in0_kernel,
        grid=(t // TM,),
        in_specs=[_rows(D_MODEL), _full((D_MODEL, n_in))],
        out_specs=[_rows(A_WIDTH), _rows(A_WIDTH), _rows(A_WIDTH), _rows(B_WIDTH)],
        out_shape=[jax.ShapeDtypeStruct((t, A_WIDTH), BF16)] * 3 + [jax.ShapeDtypeStruct((t, B_WIDTH), F32)],
        compiler_params=_params(),
        name="in_proj0",
    )(x, w)


def _mix0_kernel(tiles_per_seq, q_ref, kc_ref, kp_ref, vc_ref, vp_ref, hc_ref, hp_ref, x_ref, gtab_ref,
                 cw_ref, cb_ref, cg_ref, cnb_ref, wout_ref, ng_ref, nb_ref, o_ref,
                 kband, vband, hext, hshift, cat, bias):
    seq_tile = pl.program_id(0) % tiles_per_seq
    at_start = seq_tile == 0

    @pl.when(pl.program_id(0) == 0)
    def _():
        for h in range(A_HEADS):
            g = jnp.broadcast_to(gtab_ref[h:h + 1, :] * LOG2E, (CHUNK, GTAB_LEN))
            toeplitz = pltpu.roll(g, GTAB_LEN - (CHUNK - 1), 1, stride=1, stride_axis=0)
            bias[h // 2, (h % 2) * CHUNK:(h % 2 + 1) * CHUNK, :] = toeplitz[:, :A_BAND]

    kband[0:A_LEFT, :] = kp_ref[...]
    kband[A_LEFT:, :] = kc_ref[...]
    vband[0:A_LEFT, :] = vp_ref[...]
    vband[A_LEFT:, :] = vc_ref[...]
    hext[0:CONV_PREV, :] = jnp.where(at_start, 0.0, hp_ref[...])
    hext[CONV_PREV:, :] = hc_ref[...]

    col = lax.broadcasted_iota(jnp.int32, (2 * CHUNK, A_BAND), 1)
    low_half = lax.broadcasted_iota(jnp.int32, (CHUNK, LANES), 1) < A_HEAD_DIM
    scale2 = A_HEAD_DIM ** -0.5 * LOG2E
    pairs = [slice(p * LANES, (p + 1) * LANES) for p in range(A_HEADS // 2)]

    def attend(masked):
        def chunk_body(c, carry):
            r0 = pl.multiple_of(c * CHUNK, CHUNK)
            valid = col >= A_LEFT - c * CHUNK
            scores = []
            for ls in pairs:
                qp = q_ref[pl.ds(r0, CHUNK), ls]
                zero = jnp.zeros_like(qp)
                qs = jnp.concatenate([jnp.where(low_half, qp, zero), jnp.where(low_half, zero, qp)], axis=0)
                scores.append(_dot_nt(qs, kband[pl.ds(r0, A_BAND), ls]))
            probs, denoms = [], []
            for p, s in enumerate(scores):
                s = s * scale2 + bias[p]
                if masked:
                    s = jnp.where(valid, s, NEG_INF)
                e = jnp.exp2(s - jnp.max(s, axis=-1, keepdims=True))
                denoms.append(jnp.sum(e, axis=-1, keepdims=True))
                probs.append(e.astype(BF16))
            outs = []
            for ls, e, l in zip(pairs, probs, denoms):
                o = _dot(e, vband[pl.ds(r0, A_BAND), ls]) / l
                outs.append(jnp.where(low_half, o[:CHUNK], o[CHUNK:]))
            cat[pl.ds(r0, CHUNK), 0:A_WIDTH] = jnp.concatenate(outs, axis=1).astype(BF16)
            return carry

        lax.fori_loop(0, CHUNKS_PER_TILE, chunk_body, 0)

    @pl.when(at_start)
    def _():
        attend(True)

    @pl.when(jnp.logical_not(at_start))
    def _():
        attend(False)

    for b in range(1, 8):
        hshift[b - 1] = hext[b:b + HSHIFT_ROWS, :]
    off = CONV_PREV - (CONV_WIDTH - 1)
    for rb in range(CHUNKS_PER_TILE):
        acc = cb_ref[...]
        for w in range(CONV_WIDTH):
            shift, base = (off + w) % 8, rb * CHUNK + (off + w) // 8 * 8
            src = hext if shift == 0 else hshift.at[shift - 1]
            acc = acc + src[base:base + CHUNK, :] * cw_ref[w:w + 1, :]
        y = _layer_norm(acc, cg_ref[...], cnb_ref[...])
        y = y * jax.nn.sigmoid(y)
        cat[rb * CHUNK:(rb + 1) * CHUNK, A_WIDTH:] = y.astype(BF16)

    m = _dot(cat[...], wout_ref[...])
    o_ref[...] = _layer_norm(ALPHA * x_ref[...] + m, ng_ref[...], nb_ref[...])


def _mix0(q, k, v, h, x, gtab, cw, cb, cg, cnb, wout, ng, nb, tiles_per_seq):
    t = x.shape[0]
    prev = pl.BlockSpec((TM, A_WIDTH), lambda i: (jnp.maximum(i - 1, 0), 0))
    hprev = pl.BlockSpec((CONV_PREV, B_WIDTH), lambda i: (jnp.maximum(i * (TM // CONV_PREV) - 1, 0), 0))
    return pl.pallas_call(
        functools.partial(_mix0_kernel, tiles_per_seq),
        grid=(t // TM,),
        in_specs=[_rows(A_WIDTH), _rows(A_WIDTH), prev, _rows(A_WIDTH), prev, _rows(B_WIDTH), hprev,
                  _rows(D_MODEL), _full(gtab.shape), _full(cw.shape), _full(cb.shape), _full(cg.shape),
                  _full(cnb.shape), _full(wout.shape), _full(ng.shape), _full(nb.shape)],
        out_specs=_rows(D_MODEL),
        out_shape=jax.ShapeDtypeStruct((t, D_MODEL), F32),
        scratch_shapes=[pltpu.VMEM((A_LEFT + TM, A_WIDTH), BF16), pltpu.VMEM((A_LEFT + TM, A_WIDTH), BF16),
                        pltpu.VMEM((CONV_PREV + TM, B_WIDTH), F32), pltpu.VMEM((7, HSHIFT_ROWS, B_WIDTH), F32),
                        pltpu.VMEM((TM, D_MODEL), BF16), pltpu.VMEM((A_HEADS // 2, 2 * CHUNK, A_BAND), F32)],
        compiler_params=_params("arbitrary"),
        name="mix0",
    )(q, k, k, v, v, h, h, x, gtab, cw, cb, cg, cnb, wout, ng, nb)


def _ffn_kernel(x_ref, wg_ref, wu_ref, wd_ref, g_ref, b_ref, o_ref, hm):
    x = x_ref[...]
    xb = x.astype(BF16)
    for c in range(D_FF // FF_CHUNK):
        sl = slice(c * FF_CHUNK, (c + 1) * FF_CHUNK)
        gate = _dot(xb, wg_ref[:, sl])
        up = _dot(xb, wu_ref[:, sl])
        hm[:, sl] = (gate * jax.nn.sigmoid(gate) * up).astype(BF16)
    f = _dot(hm[...], wd_ref[...])
    o_ref[...] = _layer_norm(ALPHA * x + f, g_ref[...], b_ref[...])


def _ffn(x, wg, wu, wd, g, b):
    t = x.shape[0]
    return pl.pallas_call(
        _ffn_kernel,
        grid=(t // TM,),
        in_specs=[_rows(D_MODEL), _full(wg.shape), _full(wu.shape), _full(wd.shape), _full(g.shape), _full(b.shape)],
        out_specs=_rows(D_MODEL),
        out_shape=jax.ShapeDtypeStruct((t, D_MODEL), F32),
        scratch_shapes=[pltpu.VMEM((TM, D_FF), BF16)],
        compiler_params=_params(),
        name="ffn",
    )(x, wg, wu, wd, g, b)


def _in1_kernel(x_ref, w_ref, wz_ref, gw_ref, gb_ref, q_ref, k_ref, v_ref, g_ref, la_ref):
    xb = x_ref[...].astype(BF16)

    def proj(lo, width):
        return _dot(xb, w_ref[:, lo:lo + width])

    q_ref[...] = proj(0, C_KEY_DIM)
    k_ref[...] = proj(C_KEY_DIM, C_KEY_DIM)
    for c in range(2):
        half = C_VAL_DIM // 2
        v_ref[:, c * half:(c + 1) * half] = proj(2 * C_KEY_DIM + c * half, half).astype(BF16)
        g_ref[:, c * half:(c + 1) * half] = proj(2 * C_KEY_DIM + C_VAL_DIM + c * half, half)
    z = _dot(xb, wz_ref[...])
    pre = _dot(z.astype(BF16), gw_ref[...]) + gb_ref[...]
    log_sig = jnp.minimum(pre, 0.0) - jnp.log1p(jnp.exp(-jnp.abs(pre)))
    la_ref[...] = log_sig / GATE_TEMP


def _in1(x, w, wz, gw, gb):
    t = x.shape[0]
    return pl.pallas_call(
        _in1_kernel,
        grid=(t // TM,),
        in_specs=[_rows(D_MODEL), _full(w.shape), _full(wz.shape), _full(gw.shape), _full(gb.shape)],
        out_specs=[_rows(C_KEY_DIM), _rows(C_KEY_DIM), _rows(C_VAL_DIM), _rows(C_VAL_DIM), _rows(C_KEY_DIM)],
        out_shape=[jax.ShapeDtypeStruct((t, C_KEY_DIM), F32), jax.ShapeDtypeStruct((t, C_KEY_DIM), F32),
                   jax.ShapeDtypeStruct((t, C_VAL_DIM), BF16), jax.ShapeDtypeStruct((t, C_VAL_DIM), F32),
                   jax.ShapeDtypeStruct((t, C_KEY_DIM), F32)],
        compiler_params=_params(),
        name="in_proj1",
    )(x, w, wz, gw, gb)


def _gla_constants():
    idx = np.arange(CHUNK)
    i, t = idx[:, None], idx[None, :]
    sel = [t <= i]
    for lvl in GLA_SEL_LEVELS:
        s = 1 << lvl
        sel.append((t >= i - i % s) & (t <= i))
        sel.append((t > i) & (t <= i - i % s + s - 1))
    sel = np.concatenate(sel, axis=0).astype(np.float32)
    sel = np.concatenate([sel] * 3, axis=1)
    masks = [i == t]
    for lvl in range(GLA_LEVELS):
        bi, bj = i >> lvl, t >> lvl
        masks.append((bi % 2 == 1) & (bj == bi - 1))
    masks.append(np.zeros_like(masks[0]))
    pairs = [np.concatenate(masks[p:p + 2], axis=1) for p in range(0, len(masks), 2)]
    masks = np.stack(pairs).astype(np.float32)
    return jnp.asarray(sel, BF16), jnp.asarray(masks, F32)


def _gla_kernel(tiles_per_seq, q_ref, k_ref, v_ref, g_ref, la_ref, x_ref, sel_ref, mask_ref, hg_ref,
                wout_ref, ng_ref, nb_ref, o_ref, state, obuf, dsum):
    @pl.when(pl.program_id(0) % tiles_per_seq == 0)
    def _():
        state[...] = jnp.zeros_like(state)

    q_scale = C_DK ** -0.5
    n_groups = CHUNK // SUBLANES
    heads = [(slice(h * C_DK, (h + 1) * C_DK), slice(h * C_DV, (h + 1) * C_DV)) for h in range(C_HEADS)]

    def decay_sums(c):
        la2 = la_ref[pl.ds(pl.multiple_of(c * CHUNK, CHUNK), CHUNK), :] * LOG2E
        hi = la2.astype(BF16)
        rest = la2 - hi.astype(F32)
        mid = rest.astype(BF16)
        lo = (rest - mid.astype(F32)).astype(BF16)
        return _dot(sel_ref[...], jnp.concatenate([hi, mid, lo], axis=0))

    dsum[...] = decay_sums(0)

    def chunk_body(c, carry):
        r0 = pl.multiple_of(c * CHUNK, CHUNK)
        rows = pl.ds(r0, CHUNK)
        b = dsum[0:CHUNK, :]
        dq, dk = {0: la_ref[rows, :] * LOG2E}, {}
        for n, lvl in enumerate(GLA_SEL_LEVELS):
            dq[lvl] = dsum[(2 * n + 1) * CHUNK:(2 * n + 2) * CHUNK, :]
            dk[lvl] = dsum[(2 * n + 2) * CHUNK:(2 * n + 3) * CHUNK, :]
        grp = [b[g * SUBLANES:(g + 1) * SUBLANES] for g in range(n_groups)]
        end = [jnp.broadcast_to(b[(g + 1) * SUBLANES - 1:(g + 1) * SUBLANES], grp[g].shape) for g in range(n_groups)]
        for lvl in range(GLA_SEL_LEVELS[-1] + 1, GLA_LEVELS + 1):
            per = (1 << lvl) // SUBLANES
            first = [g - g % per for g in range(n_groups)]
            dq[lvl] = jnp.concatenate(
                [grp[g] if first[g] == 0 else grp[g] - end[first[g] - 1] for g in range(n_groups)], axis=0)
            dk[lvl] = jnp.concatenate([end[first[g] + per - 1] - grp[g] for g in range(n_groups)], axis=0)

        q = q_ref[rows, :] * q_scale
        k = k_ref[rows, :]
        kb = k.astype(BF16)
        q_ops = [q.astype(BF16)] + [(q * jnp.exp2(dq[lvl])).astype(BF16) for lvl in range(GLA_LEVELS)]
        k_ops = [kb, kb] + [(k * jnp.exp2(dk[lvl])).astype(BF16) for lvl in range(1, GLA_LEVELS)]
        q_cum = (q * jnp.exp2(b)).astype(BF16)
        k_dec = (k * jnp.exp2(dk[GLA_LEVELS])).astype(BF16)
        decay = jnp.exp2(b[CHUNK - 1:CHUNK, :])
        zero = jnp.zeros((CHUNK, C_DK), BF16)

        products = []
        for ks, _ in heads:
            per_head = []
            for p in range(0, len(q_ops) - 1, 2):
                lhs = jnp.concatenate([q_ops[p][:, ks], q_ops[p + 1][:, ks]], axis=1)
                rhs = jnp.concatenate([jnp.concatenate([k_ops[p][:, ks], zero], axis=1),
                                       jnp.concatenate([zero, k_ops[p + 1][:, ks]], axis=1)], axis=0)
                per_head.append(_dot_nt(lhs, rhs))
            per_head.append(_dot_nt(q_ops[-1][:, ks], jnp.concatenate([k_ops[-1][:, ks], zero], axis=0)))
            products.append(per_head)
        attn = []
        for per_head in products:
            a = mask_ref[0] * per_head[0]
            for p in range(1, len(per_head)):
                a = a + mask_ref[p] * per_head[p]
            attn.append(a.astype(BF16))

        vals = [v_ref[rows, vs] for _, vs in heads]
        olds = [state[h] for h in range(C_HEADS)]
        outs = [_dot(attn[h], jnp.concatenate([vals[h], vals[h]], axis=0))
                + _dot_nt(q_cum[:, ks], olds[h].astype(BF16)) for h, (ks, _) in enumerate(heads)]
        for h, (ks, _) in enumerate(heads):
            state[h] = olds[h] * decay[:, ks] + _dot_tn(vals[h], k_dec[:, ks])
        for h, (_, vs) in enumerate(heads):
            o = outs[h]
            o = o * lax.rsqrt(jnp.mean(o * o, axis=-1, keepdims=True) + RMS_EPS) * hg_ref[:, vs]
            gh = g_ref[rows, vs]
            obuf[rows, vs] = (o * (gh * jax.nn.sigmoid(gh))).astype(BF16)
        dsum[...] = decay_sums(jnp.minimum(c + 1, CHUNKS_PER_TILE - 1))
        return carry

    lax.fori_loop(0, CHUNKS_PER_TILE, chunk_body, 0)

    m = _dot(obuf[...], wout_ref[...])
    o_ref[...] = _layer_norm(ALPHA * x_ref[...] + m, ng_ref[...], nb_ref[...])


def _gla(q, k, v, g, la, x, hg, wout, ng, nb, tiles_per_seq):
    t = x.shape[0]
    sel, masks = _gla_constants()
    return pl.pallas_call(
        functools.partial(_gla_kernel, tiles_per_seq),
        grid=(t // TM,),
        in_specs=[_rows(C_KEY_DIM), _rows(C_KEY_DIM), _rows(C_VAL_DIM), _rows(C_VAL_DIM), _rows(C_KEY_DIM),
                  _rows(D_MODEL), _full(sel.shape), _full(masks.shape), _full(hg.shape), _full(wout.shape),
                  _full(ng.shape), _full(nb.shape)],
        out_specs=_rows(D_MODEL),
        out_shape=jax.ShapeDtypeStruct((t, D_MODEL), F32),
        scratch_shapes=[pltpu.VMEM((C_HEADS, C_DV, C_DK), F32), pltpu.VMEM((TM, C_VAL_DIM), BF16),
                        pltpu.VMEM((GLA_SEL_ROWS, C_KEY_DIM), F32)],
        compiler_params=_params("arbitrary"),
        name="gla",
    )(q, k, v, g, la, x, sel, masks, hg, wout, ng, nb)


def _rel_bias_diagonals(rel_bias):
    far = jnp.broadcast_to(rel_bias[:, 2 * REL_CLIP:], (rel_bias.shape[0], A_BAND - REL_CLIP))
    near = rel_bias[:, REL_CLIP - (CHUNK - 1):2 * REL_CLIP][:, ::-1]
    return jnp.concatenate([far, near, near[:, -1:]], axis=1)


def kernel(x, even_w_in, even_rel_bias, even_conv_w, even_conv_b, even_conv_norm_g, even_conv_norm_b, even_w_out, odd_w_in, odd_gate_w, odd_gate_b, odd_head_norm_g, odd_w_out, mix_norm_g, mix_norm_b, ffn_w_gate, ffn_w_up, ffn_w_down, ffn_norm_g, ffn_norm_b):
    bsz, seq, d = x.shape
    assert d == D_MODEL and seq % TM == 0
    tiles_per_seq = seq // TM
    row = lambda p: p.reshape(1, -1)
    xt = x.reshape(bsz * seq, d)

    q, k, v, h = _in0(xt, even_w_in[0].astype(BF16))
    xt = _mix0(q, k, v, h, xt, _rel_bias_diagonals(even_rel_bias[0]), even_conv_w[0], row(even_conv_b[0]),
               row(even_conv_norm_g[0]), row(even_conv_norm_b[0]), even_w_out[0].astype(BF16),
               row(mix_norm_g[0]), row(mix_norm_b[0]), tiles_per_seq)
    xt = _ffn(xt, ffn_w_gate[0].astype(BF16), ffn_w_up[0].astype(BF16), ffn_w_down[0].astype(BF16),
              row(ffn_norm_g[0]), row(ffn_norm_b[0]))

    n_main = 2 * C_KEY_DIM + 2 * C_VAL_DIM
    w1 = odd_w_in[0]
    wz = jnp.pad(w1[:, n_main:], ((0, 0), (0, LANES - GATE_RANK))).astype(BF16)
    gw = jnp.pad(odd_gate_w[0], ((0, LANES - GATE_RANK), (0, 0))).astype(BF16)
    q, k, v, g, la = _in1(xt, w1[:, :n_main].astype(BF16), wz, gw, row(odd_gate_b[0]))
    xt = _gla(q, k, v, g, la, xt, row(odd_head_norm_g[0]), odd_w_out[0].astype(BF16),
              row(mix_norm_g[1]), row(mix_norm_b[1]), tiles_per_seq)
    xt = _ffn(xt, ffn_w_gate[1].astype(BF16), ffn_w_up[1].astype(BF16), ffn_w_down[1].astype(BF16),
              row(ffn_norm_g[1]), row(ffn_norm_b[1]))
    return xt.reshape(bsz, seq, d)
```
